```python
import math
import jax
import jax.numpy as jnp
from jax import lax
import numpy as np

D_MODEL = 2048
BATCH = 4
SEQ = 2048
DEPTH = 1
DEC_BATCH = 128
DEC_SEQ = 8
PAST_LEN = 16384
PAGE_SIZE = 128

D_SSM = D_MODEL // 2
SSM_GROUP = 16
SSM_GROUPS = D_SSM // SSM_GROUP
SSM_STATE = 64
D_MLSTM = D_MODEL // 2
MLSTM_HEADS = 4
MLSTM_HEAD_DIM = D_MLSTM // MLSTM_HEADS
CONV_W = 4
MLSTM_CHUNK = 64
D_MIX = D_SSM + D_MLSTM
N_IN = D_SSM + 3 * D_MLSTM + 2 * MLSTM_HEADS
N_EXPERTS = 32
TOP_K = 4
D_FF = D_MODEL
SWIGLU_LIMIT = 7.0
SWIGLU_ALPHA = 1.702
MOE_BLOCK = 128
DN_ALPHA = (2 * DEPTH) ** 0.25
DN_BETA = (8 * DEPTH) ** -0.25
LN_EPS = 1e-5

kernel_name = 'hybrid_s5_mlstm_moe_step'

F32 = jnp.float32


def layer_norm(x, g=None, b=None):
    xf = x.astype(F32)
    xc = xf - jnp.mean(xf, axis=-1, keepdims=True)
    y = xc * lax.rsqrt(jnp.mean(xc * xc, axis=-1, keepdims=True) + LN_EPS)
    if g is not None:
        y = y * g.astype(F32) + b.astype(F32)
    return y


def s5_scan(u, h0_re, h0_im, lam_re, lam_im, log_dt, b_re, b_im, c_re, c_im, d_skip):
    L = u.shape[1]
    lam = lax.complex(lam_re.astype(F32), lam_im.astype(F32))
    lam_dt = lam * jnp.exp(log_dt.astype(F32))[:, None]
    a_bar = jnp.exp(lam_dt)
    b_bar = ((a_bar - 1.0) / lam)[:, :, None] * lax.complex(b_re.astype(F32), b_im.astype(F32))
    bu = jnp.einsum('blgi,gpi->blgp', u.astype(jnp.complex64), b_bar)

    def combine(e1, e2):
        a1, x1 = e1
        a2, x2 = e2
        return a1 * a2, a2 * x1 + x2

    _, h = lax.associative_scan(combine, (jnp.broadcast_to(a_bar, bu.shape), bu), axis=1)
    t = jnp.arange(1, L + 1, dtype=F32)
    h0 = lax.complex(h0_re.astype(F32), h0_im.astype(F32))
    h = h + jnp.exp(lam_dt[None] * t[:, None, None])[None] * h0[:, None]
    c = lax.complex(c_re.astype(F32), c_im.astype(F32))
    y = jnp.real(jnp.einsum('blgp,gip->blgi', h, c)) + d_skip.astype(F32).reshape(SSM_GROUPS, SSM_GROUP) * u
    h_last = h[:, -1]
    return y, jnp.real(h_last), jnp.imag(h_last)


def mlstm_chunkwise(q, k, v, i_pre, f_pre, c0, n0, m0):
    bt, nh, L, dh = q.shape
    lc = MLSTM_CHUNK if L % MLSTM_CHUNK == 0 else L
    nc = L // lc

    def chunks(a):
        return jnp.moveaxis(a.reshape(bt, nh, nc, lc, *a.shape[3:]), 2, 0)

    logf = jax.nn.log_sigmoid(f_pre)
    causal = jnp.tril(jnp.ones((lc, lc), dtype=bool))

    def step(carry, inp):
        c, n, m = carry
        qc, kc, vc, ic, fc = inp
        b = jnp.cumsum(fc, axis=-1)
        inter_log = b + m[..., None]
        d_log = jnp.where(causal, b[..., :, None] - b[..., None, :] + ic[..., None, :], -jnp.inf)
        m_t = jnp.maximum(inter_log, jnp.max(d_log, axis=-1))
        inter_w = jnp.exp(inter_log - m_t)
        s = jnp.einsum('bhtd,bhsd->bhts', qc, kc) * jnp.exp(d_log - m_t[..., None])
        num = inter_w[..., None] * jnp.einsum('bhtd,bhde->bhte', qc, c) + jnp.einsum('bhts,bhse->bhte', s, vc)
        den = inter_w * jnp.einsum('bhtd,bhd->bht', qc, n) + jnp.sum(s, axis=-1)
        h = num / jnp.maximum(jnp.abs(den), jnp.exp(-m_t))[..., None]
        m_new = m_t[..., -1]
        decay = jnp.exp(b[..., -1] + m - m_new)
        w_s = jnp.exp(b[..., -1:] - b + ic - m_new[..., None])
        kw = kc * w_s[..., None]
        c_new = decay[..., None, None] * c + jnp.einsum('bhsd,bhse->bhde', kw, vc)
        n_new = decay[..., None] * n + jnp.sum(kw, axis=2)
        return (c_new, n_new, m_new), h

    (c, n, m), h = lax.scan(step, (c0, n0, m0),
                            (chunks(q), chunks(k), chunks(v), chunks(i_pre), chunks(logf)))
    h = jnp.moveaxis(h, 0, 2).reshape(bt, nh, L, dh)
    return h, c, n, m


def moe(h, p):
    T = h.shape[0]
    logits = h @ p['w_router'].astype(F32) + p['b_router'].astype(F32)
    top_val, top_idx = lax.top_k(logits, TOP_K)
    gates = jax.nn.softmax(top_val, axis=-1)
    M = T * TOP_K
    flat_e = top_idx.reshape(M)
    order = jnp.argsort(flat_e)
    e_sorted = flat_e[order]
    tok_sorted = order // TOP_K
    g_sorted = gates.reshape(M)[order]
    counts = jnp.zeros((N_EXPERTS,), jnp.int32).at[flat_e].add(1)
    padded = (counts + MOE_BLOCK - 1) // MOE_BLOCK * MOE_BLOCK
    padded_end = jnp.cumsum(padded)
    padded_start = padded_end - padded
    start = jnp.cumsum(counts) - counts
    dest = padded_start[e_sorted] + jnp.arange(M, dtype=jnp.int32) - start[e_sorted]
    n_blocks = -(-M // MOE_BLOCK) + N_EXPERTS
    x_pad = jnp.zeros((n_blocks * MOE_BLOCK, D_MODEL), F32).at[dest].set(h[tok_sorted])
    blk_e = jnp.minimum(jnp.searchsorted(padded_end, jnp.arange(n_blocks, dtype=jnp.int32) * MOE_BLOCK,
                                         side='right'), N_EXPERTS - 1)

    def expert_block(args):
        xb, e = args
        a = xb @ p['w_gate'][e].astype(F32) + p['b_gate'][e].astype(F32)
        lin = xb @ p['w_up'][e].astype(F32) + p['b_up'][e].astype(F32)
        a = jnp.minimum(a, SWIGLU_LIMIT)
        lin = jnp.clip(lin, -SWIGLU_LIMIT, SWIGLU_LIMIT)
        act = (lin + 1.0) * (a * jax.nn.sigmoid(SWIGLU_ALPHA * a))
        return act @ p['w_down'][e].astype(F32) + p['b_down'][e].astype(F32)

    out = lax.map(expert_block, (x_pad.reshape(n_blocks, MOE_BLOCK, D_MODEL), blk_e))
    out = out.reshape(n_blocks * MOE_BLOCK, D_MODEL)
    return jnp.zeros((T, D_MODEL), F32).at[tok_sorted].add(out[dest] * g_sorted[:, None])


def hybrid_layer(x, cond, ssm_re0, ssm_im0, mem_c0, mem_n0, mem_m0, conv0, p):
    bt, L, _ = x.shape
    mod = jax.nn.silu(cond.astype(F32)) @ p['w_ada'].astype(F32) + p['b_ada'].astype(F32)
    sh1, sc1, g1, sh2, sc2, g2 = [m_[:, None, :] for m_ in jnp.split(mod, 6, axis=-1)]

    h = layer_norm(x) * (1.0 + sc1) + sh1
    proj = h @ p['w_in'].astype(F32)
    o1 = D_SSM
    o2 = o1 + D_MLSTM
    o3 = o2 + D_MLSTM
    o4 = o3 + D_MLSTM
    o5 = o4 + MLSTM_HEADS
    u, qk_in, v_in = proj[..., :o1], proj[..., o1:o2], proj[..., o2:o3]
    o_in, i_in, f_in = proj[..., o3:o4], proj[..., o4:o5], proj[..., o5:]

    y_ssm, ssm_re, ssm_im = s5_scan(u.reshape(bt, L, SSM_GROUPS, SSM_GROUP), ssm_re0, ssm_im0,
                                    p['lam_re'], p['lam_im'], p['log_dt'], p['ssm_B_re'], p['ssm_B_im'],
                                    p['ssm_C_re'], p['ssm_C_im'], p['ssm_D'])
    z = jax.nn.gelu(y_ssm.reshape(bt, L, D_SSM))
    ssm_out = z * jax.nn.sigmoid(z @ p['w_glu'].astype(F32) + p['b_glu'].astype(F32))

    xpad = jnp.concatenate([conv0.astype(F32), qk_in], axis=1)
    w_conv = p['w_conv'].astype(F32)
    conv = p['b_conv'].astype(F32) + xpad[:, :L] * w_conv[0]
    for j in range(1, CONV_W):
        conv = conv + xpad[:, j:j + L] * w_conv[j]
    new_conv = xpad[:, L:]
    xc = jax.nn.silu(conv)

    def heads(a):
        return a.reshape(bt, L, MLSTM_HEADS, MLSTM_HEAD_DIM).transpose(0, 2, 1, 3)

    q = jnp.einsum('bhld,hde->bhle', heads(xc), p['w_q'].astype(F32))
    k = jnp.einsum('bhld,hde->bhle', heads(xc), p['w_k'].astype(F32)) * (MLSTM_HEAD_DIM ** -0.5)
    v = heads(v_in)
    i_pre = (i_in + p['b_igate'].astype(F32)).transpose(0, 2, 1)
    f_pre = (f_in + p['b_fgate'].astype(F32)).transpose(0, 2, 1)
    h_cell, mem_c, mem_n, mem_m = mlstm_chunkwise(q, k, v, i_pre, f_pre, mem_c0.astype(F32),
                                                  mem_n0.astype(F32), mem_m0.astype(F32))
    h_cell = jax.nn.sigmoid(heads(o_in)) * h_cell
    mlstm_out = (layer_norm(h_cell).transpose(0, 2, 1, 3).reshape(bt, L, D_MLSTM) * p['gn_gain'].astype(F32)
                 + p['mlstm_skip'].astype(F32) * xc)

    mix = jnp.concatenate([ssm_out, mlstm_out], axis=-1) @ p['w_out'].astype(F32)
    x1 = layer_norm(DN_ALPHA * x.astype(F32) + g1 * mix, p['ln1_g'], p['ln1_b'])

    h2 = layer_norm(x1) * (1.0 + sc2) + sh2
    ffn = moe(h2.reshape(bt * L, D_MODEL), p).reshape(bt, L, D_MODEL)
    y = layer_norm(DN_ALPHA * x1 + g2 * ffn, p['ln2_g'], p['ln2_b'])
    return y, (ssm_re, ssm_im, mem_c, mem_n, mem_m, new_conv)


def setup_inputs(seed: int = 0) -> dict:
    key = jax.random.key(seed)
    ks = iter(jax.random.split(key, 64))

    def nrm(shape, scale):
        return scale * jax.random.normal(next(ks), shape, F32)

    H, DH, G, P = MLSTM_HEADS, MLSTM_HEAD_DIM, SSM_GROUPS, SSM_STATE
    x_prompt = nrm((BATCH, SEQ, D_MODEL), 1.0)
    x_sample = nrm((DEC_BATCH, DEC_SEQ, D_MODEL), 1.0)
    c_prompt = nrm((BATCH, D_MODEL), 1.0)
    c_sample = nrm((DEC_BATCH, D_MODEL), 1.0)
    state_ssm_re = nrm((DEPTH, DEC_BATCH, G, P), 0.1)
    state_ssm_im = nrm((DEPTH, DEC_BATCH, G, P), 0.1)
    state_mlstm_C = nrm((DEPTH, DEC_BATCH, H, DH, DH), 0.05)
    state_mlstm_n = nrm((DEPTH, DEC_BATCH, H, DH), 0.05)
    state_mlstm_m = jax.random.uniform(next(ks), (DEPTH, DEC_BATCH, H), F32, 0.0, 2.0)
    state_conv = nrm((DEPTH, DEC_BATCH, CONV_W - 1, D_MLSTM), 1.0)

    w_ada = nrm((DEPTH, D_MODEL, 6 * D_MODEL), 0.5 * D_MODEL ** -0.5)
    b_ada = nrm((DEPTH, 6 * D_MODEL), 0.02)
    v0 = D_SSM + D_MLSTM
    w_in = nrm((DEPTH, D_MODEL, N_IN), D_MODEL ** -0.5)
    w_in = w_in.at[:, :, v0:v0 + D_MLSTM].multiply(DN_BETA)
    n_idx = jnp.arange(P, dtype=F32)
    lam_re = -0.5 + nrm((DEPTH, G, P), 0.01)
    lam_im = math.pi * n_idx + nrm((DEPTH, G, P), 0.01)
    log_dt = jax.random.uniform(next(ks), (DEPTH, G), F32, math.log(1e-3), math.log(1e-1))
    ssm_B_re = nrm((DEPTH, G, P, SSM_GROUP), (2 * SSM_GROUP) ** -0.5)
    ssm_B_im = nrm((DEPTH, G, P, SSM_GROUP), (2 * SSM_GROUP) ** -0.5)
    ssm_C_re = nrm((DEPTH, G, SSM_GROUP, P), (2 * P) ** -0.5)
    ssm_C_im = nrm((DEPTH, G, SSM_GROUP, P), (2 * P) ** -0.5)
    ssm_D = nrm((DEPTH, D_SSM), 1.0)
    w_glu = nrm((DEPTH, D_SSM, D_SSM), D_SSM ** -0.5)
    b_glu = nrm((DEPTH, D_SSM), 0.02)
    w_conv = nrm((DEPTH, CONV_W, D_MLSTM), CONV_W ** -0.5)
    b_conv = nrm((DEPTH, D_MLSTM), 0.02)
    w_q = nrm((DEPTH, H, DH, DH), DH ** -0.5)
    w_k = nrm((DEPTH, H, DH, DH), DH ** -0.5)
    b_igate = nrm((DEPTH, H), 0.1)
    b_fgate = jnp.linspace(3.0, 6.0, H, dtype=F32)[None] + nrm((DEPTH, H), 0.1)
    gn_gain = 1.0 + nrm((DEPTH, D_MLSTM), 0.02)
    mlstm_skip = 1.0 + nrm((DEPTH, D_MLSTM), 0.02)
    w_out = nrm((DEPTH, D_MIX, D_MODEL), DN_BETA * D_MIX ** -0.5)
    ln1_g = 1.0 + nrm((DEPTH, D_MODEL), 0.02)
    ln1_b = nrm((DEPTH, D_MODEL), 0.02)
    w_router = nrm((DEPTH, D_MODEL, N_EXPERTS), D_MODEL ** -0.5)
    b_router = nrm((DEPTH, N_EXPERTS), 0.01)
    w_gate = nrm((DEPTH, N_EXPERTS, D_MODEL, D_FF), D_MODEL ** -0.5)
    b_gate = nrm((DEPTH, N_EXPERTS, D_FF), 0.02)
    w_up = nrm((DEPTH, N_EXPERTS, D_MODEL, D_FF), D_MODEL ** -0.5)
    b_up = nrm((DEPTH, N_EXPERTS, D_FF), 0.02)
    w_down = nrm((DEPTH, N_EXPERTS, D_FF, D_MODEL), DN_BETA * D_FF ** -0.5)
    b_down = nrm((DEPTH, N_EXPERTS, D_MODEL), 0.02)
    ln2_g = 1.0 + nrm((DEPTH, D_MODEL), 0.02)
    ln2_b = nrm((DEPTH, D_MODEL), 0.02)
    return {
        'x_prompt': x_prompt, 'x_sample': x_sample, 'c_prompt': c_prompt, 'c_sample': c_sample,
        'state_ssm_re': state_ssm_re, 'state_ssm_im': state_ssm_im, 'state_mlstm_C': state_mlstm_C,
        'state_mlstm_n': state_mlstm_n, 'state_mlstm_m': state_mlstm_m, 'state_conv': state_conv,
        'w_ada': w_ada, 'b_ada': b_ada, 'w_in': w_in, 'lam_re': lam_re, 'lam_im': lam_im, 'log_dt': log_dt,
        'ssm_B_re': ssm_B_re, 'ssm_B_im': ssm_B_im, 'ssm_C_re': ssm_C_re, 'ssm_C_im': ssm_C_im,
        'ssm_D': ssm_D, 'w_glu': w_glu, 'b_glu': b_glu, 'w_conv': w_conv, 'b_conv': b_conv,
        'w_q': w_q, 'w_k': w_k, 'b_igate': b_igate, 'b_fgate': b_fgate, 'gn_gain': gn_gain,
        'mlstm_skip': mlstm_skip, 'w_out': w_out, 'ln1_g': ln1_g, 'ln1_b': ln1_b,
        'w_router': w_router, 'b_router': b_router, 'w_gate': w_gate, 'b_gate': b_gate,
        'w_up': w_up, 'b_up': b_up, 'w_down': w_down, 'b_down': b_down, 'ln2_g': ln2_g, 'ln2_b': ln2_b,
    }


def reference(x_prompt, x_sample, c_prompt, c_sample, state_ssm_re, state_ssm_im, state_mlstm_C,
              state_mlstm_n, state_mlstm_m, state_conv, w_ada, b_ada, w_in, lam_re, lam_im, log_dt,
              ssm_B_re, ssm_B_im, ssm_C_re, ssm_C_im, ssm_D, w_glu, b_glu, w_conv, b_conv, w_q, w_k,
              b_igate, b_fgate, gn_gain, mlstm_skip, w_out, ln1_g, ln1_b, w_router, b_router,
              w_gate, b_gate, w_up, b_up, w_down, b_down, ln2_g, ln2_b):
    bp = x_prompt.shape[0]
    yp, ys = x_prompt, x_sample
    prompt_states = [[] for _ in range(6)]
    sample_states = [[] for _ in range(6)]
    for l in range(DEPTH):
        p = {
            'w_ada': w_ada[l], 'b_ada': b_ada[l], 'w_in': w_in[l], 'lam_re': lam_re[l], 'lam_im': lam_im[l],
            'log_dt': log_dt[l], 'ssm_B_re': ssm_B_re[l], 'ssm_B_im': ssm_B_im[l], 'ssm_C_re': ssm_C_re[l],
            'ssm_C_im': ssm_C_im[l], 'ssm_D': ssm_D[l], 'w_glu': w_glu[l], 'b_glu': b_glu[l],
            'w_conv': w_conv[l], 'b_conv': b_conv[l], 'w_q': w_q[l], 'w_k': w_k[l], 'b_igate': b_igate[l],
            'b_fgate': b_fgate[l], 'gn_gain': gn_gain[l], 'mlstm_skip': mlstm_skip[l], 'w_out': w_out[l],
            'ln1_g': ln1_g[l], 'ln1_b': ln1_b[l], 'w_router': w_router[l], 'b_router': b_router[l],
            'w_gate': w_gate[l], 'b_gate': b_gate[l], 'w_up': w_up[l], 'b_up': b_up[l],
            'w_down': w_down[l], 'b_down': b_down[l], 'ln2_g': ln2_g[l], 'ln2_b': ln2_b[l],
        }
        yp, sp = hybrid_layer(
            yp, c_prompt,
            jnp.zeros((bp, SSM_GROUPS, SSM_STATE), F32), jnp.zeros((bp, SSM_GROUPS, SSM_STATE), F32),
            jnp.zeros((bp, MLSTM_HEADS, MLSTM_HEAD_DIM, MLSTM_HEAD_DIM), F32),
            jnp.zeros((bp, MLSTM_HEADS, MLSTM_HEAD_DIM), F32), jnp.zeros((bp, MLSTM_HEADS), F32),
            jnp.zeros((bp, CONV_W - 1, D_MLSTM), F32), p)
        ys, ss = hybrid_layer(ys, c_sample, state_ssm_re[l], state_ssm_im[l], state_mlstm_C[l],
                              state_mlstm_n[l], state_mlstm_m[l], state_conv[l], p)
        for lst, s in zip(prompt_states, sp):
            lst.append(s)
        for lst, s in zip(sample_states, ss):
            lst.append(s)
    ssm_re_p, ssm_im_p, mem_c_p, mem_n_p, mem_m_p, conv_p = [jnp.stack(s) for s in prompt_states]
    ssm_re_s, ssm_im_s, mem_c_s, mem_n_s, mem_m_s, conv_s = [jnp.stack(s) for s in sample_states]
    y_prompt = yp.astype(x_prompt.dtype)
    y_sample = ys.astype(x_sample.dtype)
    return (y_prompt, y_sample, ssm_re_p, ssm_im_p, mem_c_p, mem_n_p, mem_m_p, conv_p,
            ssm_re_s, ssm_im_s, mem_c_s, mem_n_s, mem_m_s, conv_s)
```

```python
import functools
import math

import jax
import jax.numpy as jnp
from jax import lax
from jax.experimental import pallas as pl
from jax.experimental.pallas import tpu as pltpu

F32 = jnp.float32
BF16 = jnp.bfloat16

LN_EPS = 1e-5
TOP_K = 4
SWIGLU_LIMIT = 7.0
SWIGLU_ALPHA = 1.702
MLSTM_PROMPT_CHUNK = 256

SUBLANES = 8
LANES = 128
MXU_DIM = 256
VMEM_LIMIT = 56 * 1024 * 1024

EXPERT_SUB_ROWS = 256
EXPERT_SUBS = 5
EXPERT_ROWS = EXPERT_SUB_ROWS * EXPERT_SUBS
EXPERT_FF_TILE = 256


def _cparams(sem):
    return pltpu.CompilerParams(dimension_semantics=sem, vmem_limit_bytes=VMEM_LIMIT)


def _const_spec(shape):
    nd = len(shape)
    return pl.BlockSpec(shape, lambda *_: (0,) * nd, pipeline_mode=pl.Buffered(1))


def _split2(a):
    hi = a.astype(BF16)
    lo = (a - hi.astype(F32)).astype(BF16)
    return hi, lo


def _dot(a, b):
    return jnp.dot(a, b, preferred_element_type=F32)


def _dot3(a, b):
    ah, al = _split2(a)
    bh, bl = _split2(b)
    return _dot(ah, bh) + _dot(ah, bl) + _dot(al, bh)


def _ln(x):
    mu = jnp.mean(x, axis=-1, keepdims=True)
    xc = x - mu
    var = jnp.mean(xc * xc, axis=-1, keepdims=True)
    return xc * lax.rsqrt(var + LN_EPS)


def _sigmoid(x):
    return 1.0 / (1.0 + jnp.exp(-x))


def _silu(x):
    return x * _sigmoid(x)


def _gelu_tanh(x):
    c = math.sqrt(2.0 / math.pi)
    return 0.5 * x * (1.0 + jnp.tanh(c * (x + 0.044715 * (x * x * x))))


def _ada_kernel(c_ref, w_ref, b_ref, o_ref):
    s = _silu(c_ref[...])
    o_ref[...] = _dot3(s, w_ref[...]) + b_ref[...]


def _ada(cond, w_ada, b_ada):
    rows, d = cond.shape
    n = w_ada.shape[1]
    tn = 512
    return pl.pallas_call(
        _ada_kernel,
        grid=(n // tn,),
        in_specs=[_const_spec((rows, d)),
                  pl.BlockSpec((d, tn), lambda j: (0, j)),
                  pl.BlockSpec((1, tn), lambda j: (0, j))],
        out_specs=pl.BlockSpec((rows, tn), lambda j: (0, j)),
        out_shape=jax.ShapeDtypeStruct((rows, n), F32),
        compiler_params=_cparams(("arbitrary",)),
        name="ada",
    )(cond, w_ada, b_ada.reshape(1, n))


def _s5prep_kernel(lr_ref, li_ref, ldt_ref, br_ref, bi_ref, ar_ref, ai_ref, bbr_ref, bbi_ref):
    lr, li = lr_ref[...], li_ref[...]
    dt = jnp.exp(ldt_ref[...])
    mag = jnp.exp(lr * dt)
    ar = mag * jnp.cos(li * dt)
    ai = mag * jnp.sin(li * dt)
    ar_ref[...] = ar
    ai_ref[...] = ai
    xr, xi = ar - 1.0, ai
    den = lr * lr + li * li
    cr = (xr * lr + xi * li) / den
    ci = (xi * lr - xr * li) / den
    cr, ci = cr[:, None, :], ci[:, None, :]
    br, bi = br_ref[...], bi_ref[...]
    bbr_ref[...] = cr * br - ci * bi
    bbi_ref[...] = cr * bi + ci * br


def _s5prep(lam_re, lam_im, log_dt, bt_re, bt_im):
    g, p = lam_re.shape
    i = bt_re.shape[1]
    return pl.pallas_call(
        _s5prep_kernel,
        out_shape=(jax.ShapeDtypeStruct((g, p), F32), jax.ShapeDtypeStruct((g, p), F32),
                   jax.ShapeDtypeStruct((g, i, p), F32), jax.ShapeDtypeStruct((g, i, p), F32)),
        name="s5prep",
    )(lam_re, lam_im, log_dt.reshape(g, 1), bt_re, bt_im)


def _s5_padded_mats(bb_re, bb_im, c_re, c_im):
    g, i, p = bb_re.shape
    ns = g // 2
    spt = MXU_DIM // (2 * i)
    eye = jnp.eye(spt, dtype=F32)

    def in_mat(bb):
        b2 = bb.reshape(ns, 2, i, p)
        z = jnp.zeros((ns, i, p), F32)
        blk = jnp.concatenate([jnp.concatenate([b2[:, 0], z], axis=-1),
                               jnp.concatenate([z, b2[:, 1]], axis=-1)], axis=1)
        m = blk.reshape(ns // spt, spt, 1, 2 * i, 2 * p) * eye.reshape(1, spt, spt, 1, 1)
        return m.reshape(ns, spt * 2 * i, 2 * p).astype(BF16)

    def out_mat(c):
        ct = jnp.swapaxes(c, 1, 2).reshape(ns, 2, p, i)
        z = jnp.zeros((ns, p, i), F32)
        blk = jnp.concatenate([jnp.concatenate([ct[:, 0], z], axis=-1),
                               jnp.concatenate([z, ct[:, 1]], axis=-1)], axis=1)
        m = blk.reshape(ns // spt, spt, 2 * p, 1, 2 * i) * eye.reshape(1, spt, 1, spt, 1)
        return m.reshape(ns, 2 * p, spt * 2 * i).astype(BF16)

    return in_mat(bb_re), in_mat(bb_im), out_mat(c_re), out_mat(-c_im)


def _pre_kernel(x_ref, sc_ref, sh_ref, w_ref, wg_ref, u_ref, qk_ref, v_ref, o_ref, g_ref):
    bt, lt, d = x_ref.shape
    h = _ln(x_ref[...]) * (1.0 + sc_ref[...]) + sh_ref[...]
    h = h.reshape(bt * lt, d)
    hb = h.astype(BF16)
    n = u_ref.shape[-1]
    for idx, ref in enumerate((u_ref, qk_ref, v_ref, o_ref)):
        ref[...] = _dot(hb, w_ref[:, idx * n:(idx + 1) * n])
    g_ref[...] = _dot3(h, wg_ref[...]).reshape(bt, lt, g_ref.shape[-1])


def _pre(x, sc, sh, w_main, w_gate, *, bt, lt, time_major):
    bx, lx, d = x.shape
    nb, nl = bx // bt, lx // lt
    rows = bt * lt
    n = w_main.shape[1] // 4
    ng = w_gate.shape[1]
    if time_major:
        big_shape, big_map = (lx, bx * n), (lambda i, j: (j, i))
    else:
        big_shape, big_map = (bx * lx, n), (lambda i, j: (i * nl + j, 0))
    big = jax.ShapeDtypeStruct(big_shape, F32)
    big_spec = pl.BlockSpec((rows, n), big_map)
    mod_spec = pl.BlockSpec((bt, 1, d), lambda i, j: (i, 0, 0))
    return pl.pallas_call(
        _pre_kernel,
        grid=(nb, nl),
        in_specs=[pl.BlockSpec((bt, lt, d), lambda i, j: (i, j, 0)), mod_spec, mod_spec,
                  _const_spec(w_main.shape), _const_spec(w_gate.shape)],
        out_specs=(big_spec, big_spec, big_spec, big_spec,
                   pl.BlockSpec((bt, lt, ng), lambda i, j: (i, j, 0))),
        out_shape=(big, big, big, big, jax.ShapeDtypeStruct((bx, lx, ng), F32)),
        compiler_params=_cparams(("arbitrary", "arbitrary")),
        name="pre",
    )(x, sc, sh, w_main, w_gate)


S5_SLAB_BLOCK = 4


def _s5_kernel(u_ref, h0r_ref, h0i_ref, ar_ref, ai_ref, bpr_ref, bpi_ref, cpr_ref, cpi_ref, d_ref,
               wglu_ref, bglu_ref, out_ref, hr_out, hi_out, sr_ref, si_ref, str_ref, sti_ref, z_ref,
               *, steps, streams, time_major):
    ns = sr_ref.shape[0]
    spt = ns // (u_ref.shape[1] // MXU_DIM)
    j = pl.program_id(1)

    @pl.when(j == 0)
    def _():
        for k in range(ns):
            str_ref[k] = h0r_ref[:, k * LANES:(k + 1) * LANES]
            sti_ref[k] = h0i_ref[:, k * LANES:(k + 1) * LANES]

    ub = u_ref[...].astype(BF16)
    for k in range(ns):
        q = k // spt
        ut = ub[:, q * MXU_DIM:(q + 1) * MXU_DIM]
        sr_ref[k] = _dot(ut, bpr_ref[k])
        si_ref[k] = _dot(ut, bpi_ref[k])

    def rows_of(t):
        if time_major:
            return pl.ds(pl.multiple_of(t * streams, streams), streams)
        return pl.ds(t, streams, stride=steps)

    for kb in range(ns // S5_SLAB_BLOCK):
        ks = [kb * S5_SLAB_BLOCK + i for i in range(S5_SLAB_BLOCK)]
        a_r = [jnp.broadcast_to(ar_ref[k], (streams, LANES)) for k in ks]
        a_i = [jnp.broadcast_to(ai_ref[k], (streams, LANES)) for k in ks]

        def step(t, carry):
            rows = rows_of(t)
            new = []
            for idx, k in enumerate(ks):
                hr, hi = carry[2 * idx], carry[2 * idx + 1]
                nr = a_r[idx] * hr - a_i[idx] * hi + sr_ref[k, rows, :]
                ni = a_r[idx] * hi + a_i[idx] * hr + si_ref[k, rows, :]
                sr_ref[k, rows, :] = nr
                si_ref[k, rows, :] = ni
                new += [nr, ni]
            return tuple(new)

        init = []
        for k in ks:
            init += [str_ref[k], sti_ref[k]]
        fin = lax.fori_loop(0, steps, step, tuple(init))
        for idx, k in enumerate(ks):
            str_ref[k] = fin[2 * idx]
            sti_ref[k] = fin[2 * idx + 1]

    for q in range(ns // spt):
        acc = d_ref[:, q * MXU_DIM:(q + 1) * MXU_DIM] * u_ref[:, q * MXU_DIM:(q + 1) * MXU_DIM]
        for kk in range(spt):
            k = q * spt + kk
            acc = acc + _dot(sr_ref[k].astype(BF16), cpr_ref[k]) + _dot(si_ref[k].astype(BF16), cpi_ref[k])
        z_ref[:, q * MXU_DIM:(q + 1) * MXU_DIM] = _gelu_tanh(acc)

    z = z_ref[...]
    out_ref[...] = z * _sigmoid(_dot(z.astype(BF16), wglu_ref[...]) + bglu_ref[...])

    @pl.when(j == pl.num_programs(1) - 1)
    def _():
        for k in range(ns):
            hr_out[:, k * LANES:(k + 1) * LANES] = str_ref[k]
            hi_out[:, k * LANES:(k + 1) * LANES] = sti_ref[k]


def _s5(u2d, h0r, h0i, a_r, a_i, bp_r, bp_i, cp_r, cp_i, d_skip, w_glu, b_glu, *, streams, steps, nb, nl,
        time_major):
    rows = streams * steps
    n = u2d.shape[1]
    ns = a_r.shape[0]
    nstate = ns * LANES
    state_spec = pl.BlockSpec((streams, nstate), lambda i, j: (i, 0))
    kern = functools.partial(_s5_kernel, steps=steps, streams=streams, time_major=time_major)
    return pl.pallas_call(
        kern,
        grid=(nb, nl),
        in_specs=[pl.BlockSpec((rows, n), lambda i, j: (i * nl + j, 0)), state_spec, state_spec,
                  _const_spec(a_r.shape), _const_spec(a_i.shape),
                  _const_spec(bp_r.shape), _const_spec(bp_i.shape),
                  _const_spec(cp_r.shape), _const_spec(cp_i.shape),
                  _const_spec((1, n)), _const_spec(w_glu.shape), _const_spec((1, n))],
        out_specs=(pl.BlockSpec((rows, n), lambda i, j: (i * nl + j, 0)), state_spec, state_spec),
        out_shape=(jax.ShapeDtypeStruct(u2d.shape, F32),
                   jax.ShapeDtypeStruct((nb * streams, nstate), F32),
                   jax.ShapeDtypeStruct((nb * streams, nstate), F32)),
        scratch_shapes=[pltpu.VMEM((ns, rows, LANES), F32), pltpu.VMEM((ns, rows, LANES), F32),
                        pltpu.VMEM((ns, streams, LANES), F32), pltpu.VMEM((ns, streams, LANES), F32),
                        pltpu.VMEM((rows, n), F32)],
        compiler_params=_cparams(("arbitrary", "arbitrary")),
        name="s5",
    )(u2d, h0r, h0i, a_r, a_i, bp_r, bp_i, cp_r, cp_i, d_skip.reshape(1, n), w_glu, b_glu.reshape(1, n))


CONV_HIST = SUBLANES


def _mlstm_kernel(qk_ref, v_ref, o_ref, g_ref, conv0_ref, c0_ref, n0_ref, m0_ref, wc_ref, bc_ref, wq_ref,
                  wk_ref, gb_ref, gn_ref, sk_ref, out_ref, c_ref, n_ref, m_ref, xbuf_ref, *, chunk, nseq):
    hd = pl.program_id(1)
    ci = pl.program_id(2)
    dh = qk_ref.shape[-1]
    cw = wc_ref.shape[0]
    nh = g_ref.shape[-1] // 2

    @pl.when(ci == 0)
    def _():
        c_ref[...] = c0_ref[...]
        n_ref[...] = n0_ref[...]
        m_ref[...] = m0_ref[...]
        xbuf_ref[:, 0:CONV_HIST, :] = jnp.zeros((nseq, CONV_HIST, dh), F32)
        xbuf_ref[:, CONV_HIST - (cw - 1):CONV_HIST, :] = conv0_ref[...]

    @pl.when(ci > 0)
    def _():
        xbuf_ref[:, 0:CONV_HIST, :] = xbuf_ref[:, chunk:chunk + CONV_HIST, :]

    row_i = lax.broadcasted_iota(jnp.int32, (chunk, chunk), 0)
    col_i = lax.broadcasted_iota(jnp.int32, (chunk, chunk), 1)
    eye = row_i == col_i
    tril = col_i <= row_i
    lane_g = lax.broadcasted_iota(jnp.int32, (1, 2 * nh), 1)
    scale = dh ** -0.5
    wc = wc_ref[...]
    bc = bc_ref[...]
    wq = wq_ref[0]
    wk = wk_ref[0]
    gn = gn_ref[...]
    sk = sk_ref[...]
    gbias = gb_ref[...]

    def seq_body(bb, carry):
        r0 = pl.multiple_of(bb * chunk, SUBLANES)
        rows = pl.ds(r0, chunk)
        xbuf_ref[bb, CONV_HIST:CONV_HIST + chunk, :] = qk_ref[rows, :]
        conv = bc
        for jx in range(cw):
            off = CONV_HIST - (cw - 1) + jx
            conv = conv + wc[jx:jx + 1, :] * xbuf_ref[bb, off:off + chunk, :]
        xc = _silu(conv)
        xcb = xc.astype(BF16)
        q = _dot(xcb, wq)
        k = _dot(xcb, wk) * scale
        v = v_ref[rows, :]
        vb = v.astype(BF16)

        g = g_ref[bb] + gbias
        i_col = jnp.sum(jnp.where(lane_g == hd, g, 0.0), axis=-1, keepdims=True)
        f_col = jnp.sum(jnp.where(lane_g == nh + hd, g, 0.0), axis=-1, keepdims=True)
        logf_col = jnp.minimum(f_col, 0.0) - jnp.log(1.0 + jnp.exp(-jnp.abs(f_col)))
        logf_row = jnp.sum(jnp.where(eye, logf_col, 0.0), axis=0, keepdims=True)
        i_row = jnp.sum(jnp.where(eye, i_col, 0.0), axis=0, keepdims=True)
        b_col = jnp.sum(jnp.where(tril, logf_row, 0.0), axis=-1, keepdims=True)
        b_row = jnp.sum(jnp.where(tril, 0.0, logf_col) + jnp.where(eye, logf_col, 0.0), axis=0, keepdims=True)

        c = c_ref[bb, 0]
        n = n_ref[bb, 0]
        m = m_ref[bb, 0]
        d_log = jnp.where(tril, b_col - b_row + i_row, -jnp.inf)
        inter_log = b_col + m
        m_t = jnp.maximum(inter_log, jnp.max(d_log, axis=-1, keepdims=True))
        inter_w = jnp.exp(inter_log - m_t)
        qb = q.astype(BF16)
        s = lax.dot_general(qb, k.astype(BF16), (((1,), (1,)), ((), ())),
                            preferred_element_type=F32) * jnp.exp(d_log - m_t)
        num = inter_w * _dot(qb, c.astype(BF16)) + _dot(s.astype(BF16), vb)
        den = inter_w * jnp.sum(q * n, axis=-1, keepdims=True) + jnp.sum(s, axis=-1, keepdims=True)
        hcell = num / jnp.maximum(jnp.abs(den), jnp.exp(-m_t))
        m_new = m_t[chunk - 1:chunk, :]
        b_last = b_col[chunk - 1:chunk, :]
        decay = jnp.exp(b_last + m - m_new)
        kw = k * jnp.exp(b_last - b_col + i_col - m_new)
        c_ref[bb, 0] = decay * c + lax.dot_general(kw.astype(BF16), vb, (((0,), (0,)), ((), ())),
                                                   preferred_element_type=F32)
        n_ref[bb, 0] = decay * n + jnp.sum(kw, axis=0, keepdims=True)
        m_ref[bb, 0] = m_new

        hc = _sigmoid(o_ref[rows, :]) * hcell
        out_ref[rows, :] = _ln(hc) * gn + sk * xc
        return carry

    lax.fori_loop(0, nseq, seq_body, 0)


def _mlstm(qk, v, o, gates, conv0, c0, n0, m0, w_conv, b_conv, w_q, w_k, gate_bias, gn_gain, skip, *,
           chunk, nseq, time_major):
    bsz, nh, dh, _ = c0.shape
    ng = gates.shape[-1]
    cw = w_conv.shape[0]
    if time_major:
        nc = qk.shape[0] // chunk
        tok_map = lambda b, h, c: (c, b * nh + h)
    else:
        nc = 1
        tok_map = lambda b, h, c: (b, h)
    rows = nseq * chunk
    tok_spec = pl.BlockSpec((rows, dh), tok_map)
    head_vec = pl.BlockSpec((1, dh), lambda b, h, c: (0, h))
    head_mat = pl.BlockSpec((1, dh, dh), lambda b, h, c: (h, 0, 0))
    c_spec = pl.BlockSpec((nseq, 1, dh, dh), lambda b, h, c: (b, h, 0, 0))
    n_spec = pl.BlockSpec((nseq, 1, 1, dh), lambda b, h, c: (b, h, 0, 0))
    m_spec = pl.BlockSpec((nseq, 1, 1, 1), lambda b, h, c: (b, h, 0, 0))
    kern = functools.partial(_mlstm_kernel, chunk=chunk, nseq=nseq)
    return pl.pallas_call(
        kern,
        grid=(bsz // nseq, nh, nc),
        in_specs=[tok_spec, tok_spec, tok_spec,
                  pl.BlockSpec((nseq, chunk, ng), lambda b, h, c: (b, c, 0)),
                  pl.BlockSpec((nseq, cw - 1, dh), lambda b, h, c: (b, 0, h)),
                  c_spec, n_spec, m_spec,
                  pl.BlockSpec((cw, dh), lambda b, h, c: (0, h)), head_vec, head_mat, head_mat,
                  pl.BlockSpec((1, ng), lambda b, h, c: (0, 0)), head_vec, head_vec],
        out_specs=(tok_spec, c_spec, n_spec, m_spec),
        out_shape=(jax.ShapeDtypeStruct(qk.shape, F32),
                   jax.ShapeDtypeStruct((bsz, nh, dh, dh), F32),
                   jax.ShapeDtypeStruct((bsz, nh, 1, dh), F32),
                   jax.ShapeDtypeStruct((bsz, nh, 1, 1), F32)),
        scratch_shapes=[pltpu.VMEM((nseq, chunk + CONV_HIST, dh), F32)],
        compiler_params=_cparams(("arbitrary", "arbitrary", "arbitrary")),
        name="mlstm",
    )(qk, v, o, gates, conv0, c0, n0.reshape(bsz, nh, 1, dh), m0.reshape(bsz, nh, 1, 1), w_conv,
      b_conv.reshape(1, -1), w_q, w_k, gate_bias, gn_gain.reshape(1, -1), skip.reshape(1, -1))


def _post_kernel(ssm_ref, ml_ref, x_ref, g1_ref, sc_ref, sh_ref, wt_ref, wb_ref, lg_ref, lb_ref, wr_ref,
                 br_ref, x1_ref, h2_ref, lo_ref, *, alpha):
    bt, lt, d = x_ref.shape
    mix = _dot(ssm_ref[...].astype(BF16), wt_ref[...]) + _dot(ml_ref[...].astype(BF16), wb_ref[...])
    mix = mix.reshape(bt, lt, d)
    x1 = _ln(alpha * x_ref[...] + g1_ref[...] * mix) * lg_ref[...] + lb_ref[...]
    x1_ref[...] = x1
    h2 = (_ln(x1) * (1.0 + sc_ref[...]) + sh_ref[...]).reshape(bt * lt, d)
    h2_ref[...] = h2.astype(BF16)
    lo_ref[...] = _dot3(h2, wr_ref[...]) + br_ref[...]


def _post(ssm, ml, x, g1, sc, sh, w_top, w_bot, ln_g, ln_b, w_router, b_router, *, bt, lt, time_major,
          alpha):
    bx, lx, d = x.shape
    nb, nl = bx // bt, lx // lt
    rows = bt * lt
    n = w_top.shape[0]
    ne = w_router.shape[1]
    tok_map = (lambda i, j: (j, i)) if time_major else (lambda i, j: (i * nl + j, 0))
    tok_spec = pl.BlockSpec((rows, n), tok_map)
    mod_spec = pl.BlockSpec((bt, 1, d), lambda i, j: (i, 0, 0))
    x_spec = pl.BlockSpec((bt, lt, d), lambda i, j: (i, j, 0))
    row_map = lambda i, j: (i * nl + j, 0)
    return pl.pallas_call(
        functools.partial(_post_kernel, alpha=alpha),
        grid=(nb, nl),
        in_specs=[tok_spec, tok_spec, x_spec, mod_spec, mod_spec, mod_spec,
                  _const_spec(w_top.shape), _const_spec(w_bot.shape), _const_spec((1, 1, d)),
                  _const_spec((1, 1, d)), _const_spec(w_router.shape), _const_spec((1, ne))],
        out_specs=(x_spec, pl.BlockSpec((rows, d), row_map), pl.BlockSpec((rows, ne), row_map)),
        out_shape=(jax.ShapeDtypeStruct(x.shape, F32),
                   jax.ShapeDtypeStruct((bx * lx, d), BF16),
                   jax.ShapeDtypeStruct((bx * lx, ne), F32)),
        compiler_params=_cparams(("arbitrary", "arbitrary")),
        name="post",
    )(ssm, ml, x, g1, sc, sh, w_top, w_bot, ln_g.reshape(1, 1, d), ln_b.reshape(1, 1, d), w_router,
      b_router.reshape(1, ne))


def _route_kernel(lo_ref, idx_ref, gate_ref, rank_ref, cnt_ref, carry_ref):
    i = pl.program_id(0)
    tm, ne = lo_ref.shape

    @pl.when(i == 0)
    def _():
        carry_ref[...] = jnp.zeros_like(carry_ref)

    lane = lax.broadcasted_iota(jnp.int32, (tm, ne), 1)
    v = lo_ref[...]
    vals, idxs, hots = [], [], []
    for _ in range(TOP_K):
        m = jnp.max(v, axis=-1, keepdims=True)
        idx = jnp.min(jnp.where(v == m, lane, ne), axis=-1, keepdims=True)
        hot = lane == idx
        v = jnp.where(hot, -jnp.inf, v)
        vals.append(m)
        idxs.append(idx)
        hots.append(hot)
    exps = [jnp.exp(m - vals[0]) for m in vals]
    tot = exps[0]
    for e in exps[1:]:
        tot = tot + e

    cnt = jnp.zeros((tm, ne), F32)
    for hot in hots:
        cnt = cnt + jnp.where(hot, 1.0, 0.0)
    r_i = lax.broadcasted_iota(jnp.int32, (tm, tm), 0)
    c_i = lax.broadcasted_iota(jnp.int32, (tm, tm), 1)
    strict = jnp.where(c_i < r_i, 1.0, 0.0).astype(BF16)
    before = _dot(strict, cnt.astype(BF16)) + carry_ref[...]

    lane_k = lax.broadcasted_iota(jnp.int32, (tm, TOP_K), 1)
    idx_o = jnp.zeros((tm, TOP_K), jnp.int32)
    gate_o = jnp.zeros((tm, TOP_K), F32)
    rank_o = jnp.zeros((tm, TOP_K), jnp.int32)
    for kx in range(TOP_K):
        sel = lane_k == kx
        rank = jnp.sum(jnp.where(hots[kx], before, 0.0), axis=-1, keepdims=True).astype(jnp.int32)
        idx_o = jnp.where(sel, idxs[kx], idx_o)
        gate_o = jnp.where(sel, exps[kx] / tot, gate_o)
        rank_o = jnp.where(sel, rank, rank_o)
    idx_ref[...] = idx_o
    gate_ref[...] = gate_o
    rank_ref[...] = rank_o
    carry_ref[...] = carry_ref[...] + jnp.sum(cnt, axis=0, keepdims=True)
    cnt_ref[...] = carry_ref[...]


def _route(logits):
    t, ne = logits.shape
    tm = 256
    tok = lambda dt: jax.ShapeDtypeStruct((t, TOP_K), dt)
    tok_spec = pl.BlockSpec((tm, TOP_K), lambda i: (i, 0))
    return pl.pallas_call(
        _route_kernel,
        grid=(t // tm,),
        in_specs=[pl.BlockSpec((tm, ne), lambda i: (i, 0))],
        out_specs=(tok_spec, tok_spec, tok_spec, pl.BlockSpec((1, ne), lambda i: (0, 0))),
        out_shape=(tok(jnp.int32), tok(F32), tok(jnp.int32), jax.ShapeDtypeStruct((1, ne), F32)),
        scratch_shapes=[pltpu.VMEM((1, ne), F32)],
        compiler_params=_cparams(("arbitrary",)),
        name="route",
    )(logits)


def _expert_kernel(sbe_ref, sbn_ref, sbx_ref, x_ref, wg_ref, wu_ref, wd_ref, bg_ref, bu_ref, bd_ref, out_ref):
    s = pl.program_id(0)
    j = pl.program_id(1)
    nsub = sbn_ref[s]

    @pl.when(nsub > 0)
    def _():
        wg = wg_ref[0].astype(BF16)
        wu = wu_ref[0].astype(BF16)
        wd = wd_ref[0].astype(BF16)
        bg = bg_ref[0]
        bu = bu_ref[0]
        bd = bd_ref[0]
        for r in range(EXPERT_SUBS):
            rows = pl.ds(r * EXPERT_SUB_ROWS, EXPERT_SUB_ROWS)

            @pl.when(r < nsub)
            def _():
                xr = x_ref[rows, :]
                a = jnp.minimum(_dot(xr, wg) + bg, SWIGLU_LIMIT)
                lin = jnp.clip(_dot(xr, wu) + bu, -SWIGLU_LIMIT, SWIGLU_LIMIT)
                act = (lin + 1.0) * (a * _sigmoid(SWIGLU_ALPHA * a))
                contrib = _dot(act.astype(BF16), wd)

                @pl.when(j == 0)
                def _():
                    out_ref[rows, :] = contrib + bd

                @pl.when(j > 0)
                def _():
                    out_ref[rows, :] += contrib

            @pl.when(jnp.logical_and(r >= nsub, j == 0))
            def _():
                out_ref[rows, :] = jnp.zeros((EXPERT_SUB_ROWS, out_ref.shape[1]), F32)


def _experts(x_pad, sb_e, sb_n, sb_x, w_gate, b_gate, w_up, b_up, w_down, b_down):
    ne, d, dff = w_gate.shape
    nsb = sb_e.shape[0]
    tf = EXPERT_FF_TILE
    nj = dff // tf

    def jj(s, j, sbn):
        return jnp.where(sbn[s] > 0, j, nj - 1)

    grid_spec = pltpu.PrefetchScalarGridSpec(
        num_scalar_prefetch=3,
        grid=(nsb, nj),
        in_specs=[pl.BlockSpec((EXPERT_ROWS, d), lambda s, j, sbe, sbn, sbx: (sbx[s], 0)),
                  pl.BlockSpec((1, d, tf), lambda s, j, sbe, sbn, sbx: (sbe[s], 0, jj(s, j, sbn))),
                  pl.BlockSpec((1, d, tf), lambda s, j, sbe, sbn, sbx: (sbe[s], 0, jj(s, j, sbn))),
                  pl.BlockSpec((1, tf, d), lambda s, j, sbe, sbn, sbx: (sbe[s], jj(s, j, sbn), 0)),
                  pl.BlockSpec((1, 1, tf), lambda s, j, sbe, sbn, sbx: (sbe[s], 0, jj(s, j, sbn))),
                  pl.BlockSpec((1, 1, tf), lambda s, j, sbe, sbn, sbx: (sbe[s], 0, jj(s, j, sbn))),
                  pl.BlockSpec((1, 1, d), lambda s, j, sbe, sbn, sbx: (sbe[s], 0, 0))],
        out_specs=pl.BlockSpec((EXPERT_ROWS, d), lambda s, j, sbe, sbn, sbx: (sbx[s], 0)),
    )
    return pl.pallas_call(
        _expert_kernel,
        grid_spec=grid_spec,
        out_shape=jax.ShapeDtypeStruct((nsb * EXPERT_ROWS, d), F32),
        compiler_params=_cparams(("arbitrary", "arbitrary")),
        name="expert",
    )(sb_e, sb_n, sb_x, x_pad, w_gate, w_up, w_down, b_gate.reshape(ne, 1, dff), b_up.reshape(ne, 1, dff),
      b_down.reshape(ne, 1, d))


def _final_kernel(x1_ref, f_ref, gate_ref, g2_ref, lg_ref, lb_ref, y_ref, *, alpha):
    bt, lt, d = x1_ref.shape
    gate = gate_ref[...]
    ffn = jnp.zeros((bt * lt, d), F32)
    for kx in range(TOP_K):
        ffn = ffn + gate[:, kx:kx + 1] * f_ref[:, kx * d:(kx + 1) * d]
    ffn = ffn.reshape(bt, lt, d)
    y_ref[...] = _ln(alpha * x1_ref[...] + g2_ref[...] * ffn) * lg_ref[...] + lb_ref[...]


def _final(x1, f_rows, gate, g2, ln_g, ln_b, *, bt, lt, row0, alpha):
    bx, lx, d = x1.shape
    nb, nl = bx // bt, lx // lt
    rows = bt * lt
    blk0 = row0 // rows
    row_map = lambda i, j: (blk0 + i * nl + j, 0)
    x_spec = pl.BlockSpec((bt, lt, d), lambda i, j: (i, j, 0))
    return pl.pallas_call(
        functools.partial(_final_kernel, alpha=alpha),
        grid=(nb, nl),
        in_specs=[x_spec, pl.BlockSpec((rows, TOP_K * d), row_map), pl.BlockSpec((rows, TOP_K), row_map),
                  pl.BlockSpec((bt, 1, d), lambda i, j: (i, 0, 0)), _const_spec((1, 1, d)),
                  _const_spec((1, 1, d))],
        out_specs=x_spec,
        out_shape=jax.ShapeDtypeStruct(x1.shape, F32),
        compiler_params=_cparams(("arbitrary", "arbitrary")),
        name="final",
    )(x1, f_rows, gate, g2, ln_g.reshape(1, 1, d), ln_b.reshape(1, 1, d))


def _expert_plan(idx, rank, counts, n_tok):
    ne = counts.shape[0]
    nsb_e = (counts + EXPERT_ROWS - 1) // EXPERT_ROWS
    sb_end = jnp.cumsum(nsb_e)
    sb_start = sb_end - nsb_e
    total = sb_end[-1]
    nsb = -(-(n_tok * TOP_K) // EXPERT_ROWS) + ne
    s = jnp.arange(nsb, dtype=jnp.int32)
    valid = s < total
    sx = jnp.where(valid, s, total - 1)
    e = jnp.minimum(jnp.searchsorted(sb_end, sx, side="right"), ne - 1).astype(jnp.int32)
    rows_in = jnp.clip(counts[e] - (sx - sb_start[e]) * EXPERT_ROWS, 0, EXPERT_ROWS)
    sb_n = jnp.where(valid, (rows_in + EXPERT_SUB_ROWS - 1) // EXPERT_SUB_ROWS, 0).astype(jnp.int32)
    dest = sb_start[idx] * EXPERT_ROWS + rank
    tok = jnp.broadcast_to(jnp.arange(n_tok, dtype=jnp.int32)[:, None], dest.shape)
    row_tok = jnp.zeros((nsb * EXPERT_ROWS,), jnp.int32).at[dest.reshape(-1)].set(tok.reshape(-1))
    return e, sb_n, sx.astype(jnp.int32), dest, row_tok


def kernel(x_prompt, x_sample, c_prompt, c_sample, state_ssm_re, state_ssm_im, state_mlstm_C, state_mlstm_n, state_mlstm_m, state_conv, w_ada, b_ada, w_in, lam_re, lam_im, log_dt, ssm_B_re, ssm_B_im, ssm_C_re, ssm_C_im, ssm_D, w_glu, b_glu, w_conv, b_conv, w_q, w_k, b_igate, b_fgate, gn_gain, mlstm_skip, w_out, ln1_g, ln1_b, w_router, b_router, w_gate, b_gate, w_up, b_up, w_down, b_down, ln2_g, ln2_b):
    depth = w_ada.shape[0]
    assert depth == 1, "single-layer trunk"
    bp, lp, d = x_prompt.shape
    bs, ls, _ = x_sample.shape
    g, p = lam_re.shape[1:]
    nh, dh = w_q.shape[1:3]
    d_ssm = ssm_D.shape[1]
    d_ml = nh * dh
    cw = w_conv.shape[1]
    alpha = (2 * depth) ** 0.25
    l0 = 0

    cond = jnp.concatenate([c_prompt, c_sample], axis=0)
    pad = (-cond.shape[0]) % SUBLANES
    cond = jnp.pad(cond, ((0, pad), (0, 0)))
    mod = _ada(cond, w_ada[l0], b_ada[l0])
    mods_p = [m[:, None, :] for m in jnp.split(mod[:bp], 6, axis=-1)]
    mods_s = [m[:, None, :] for m in jnp.split(mod[bp:bp + bs], 6, axis=-1)]

    a_re, a_im, bb_re, bb_im = _s5prep(lam_re[l0], lam_im[l0], log_dt[l0],
                                       jnp.swapaxes(ssm_B_re[l0], 1, 2), jnp.swapaxes(ssm_B_im[l0], 1, 2))
    bp_r, bp_i, cp_r, cp_i = _s5_padded_mats(bb_re, bb_im, ssm_C_re[l0], ssm_C_im[l0])
    ns = g // 2
    a_re = a_re.reshape(ns, 1, 2 * p)
    a_im = a_im.reshape(ns, 1, 2 * p)

    n_main = d_ssm + 3 * d_ml
    w_main = w_in[l0][:, :n_main].astype(BF16)
    w_gates = w_in[l0][:, n_main:]
    w_glu_b = w_glu[l0].astype(BF16)
    w_q_b = w_q[l0].astype(BF16)
    w_k_b = w_k[l0].astype(BF16)
    w_out_b = w_out[l0].astype(BF16)
    gate_bias = jnp.concatenate([b_igate[l0], b_fgate[l0]]).reshape(1, 2 * nh)

    def mixer(x, mods, ssm_r0, ssm_i0, c0, n0, m0, conv0, *, prompt):
        sh1, sc1, g1, sh2, sc2, _ = mods
        bx, lx, _ = x.shape
        if prompt:
            bt, lt = 1, 256
        else:
            bt, lt = 32, lx
        u, qk, v, o, gates = _pre(x, sc1, sh1, w_main, w_gates, bt=bt, lt=lt, time_major=prompt)
        if prompt:
            steps = 128
            u2d = u.reshape(lx * bx, d_ssm)
            ssm, hr, hi = _s5(u2d, ssm_r0.reshape(bx, g * p), ssm_i0.reshape(bx, g * p), a_re, a_im, bp_r,
                              bp_i, cp_r, cp_i, ssm_D[l0], w_glu_b, b_glu[l0], streams=bx, steps=steps,
                              nb=1, nl=lx // steps, time_major=True)
            ssm = ssm.reshape(lx, bx * d_ssm)
            chunk, nseq = MLSTM_PROMPT_CHUNK, 1
        else:
            streams = 64
            ssm, hr, hi = _s5(u, ssm_r0.reshape(bx, g * p), ssm_i0.reshape(bx, g * p), a_re, a_im, bp_r,
                              bp_i, cp_r, cp_i, ssm_D[l0], w_glu_b, b_glu[l0], streams=streams, steps=lx,
                              nb=bx // streams, nl=1, time_major=False)
            chunk, nseq = lx, 8
        ml, c_new, n_new, m_new = _mlstm(qk, v, o, gates, conv0, c0, n0, m0, w_conv[l0], b_conv[l0], w_q_b,
                                         w_k_b, gate_bias, gn_gain[l0], mlstm_skip[l0], chunk=chunk,
                                         nseq=nseq, time_major=prompt)
        x1, h2, logits = _post(ssm, ml, x, g1, sc2, sh2, w_out_b[:d_ssm], w_out_b[d_ssm:], ln1_g[l0],
                               ln1_b[l0], w_router[l0], b_router[l0], bt=bt, lt=lt, time_major=prompt,
                               alpha=alpha)
        if prompt:
            new_conv = jnp.swapaxes(qk.reshape(lx, bx, d_ml)[lx - (cw - 1):], 0, 1)
        else:
            new_conv = qk.reshape(bx, lx, d_ml)[:, lx - (cw - 1):]
        states = (hr.reshape(bx, g, p), hi.reshape(bx, g, p), c_new, n_new.reshape(bx, nh, dh),
                  m_new.reshape(bx, nh), new_conv)
        return x1, h2, logits, states

    zeros = lambda *s: jnp.zeros(s, F32)
    x1_p, h2_p, lo_p, st_p = mixer(x_prompt, mods_p, zeros(bp, g, p), zeros(bp, g, p), zeros(bp, nh, dh, dh),
                                   zeros(bp, nh, dh), zeros(bp, nh), zeros(bp, cw - 1, d_ml), prompt=True)
    x1_s, h2_s, lo_s, st_s = mixer(x_sample, mods_s, state_ssm_re[l0], state_ssm_im[l0], state_mlstm_C[l0],
                                   state_mlstm_n[l0], state_mlstm_m[l0], state_conv[l0], prompt=False)

    n_p = bp * lp
    n_tok = n_p + bs * ls
    h2 = jnp.concatenate([h2_p, h2_s], axis=0)
    logits = jnp.concatenate([lo_p, lo_s], axis=0)
    idx, gate, rank, counts = _route(logits)
    sb_e, sb_n, sb_x, dest, row_tok = _expert_plan(idx, rank, counts[0].astype(jnp.int32), n_tok)
    x_pad = jnp.take(h2, row_tok, axis=0)
    out_pad = _experts(x_pad, sb_e, sb_n, sb_x, w_gate[l0], b_gate[l0], w_up[l0], b_up[l0], w_down[l0],
                       b_down[l0])
    f_rows = jnp.take(out_pad, dest.reshape(-1), axis=0).reshape(n_tok, TOP_K * d)

    y_p = _final(x1_p, f_rows, gate, mods_p[5], ln2_g[l0], ln2_b[l0], bt=1, lt=256, row0=0, alpha=alpha)
    y_s = _final(x1_s, f_rows, gate, mods_s[5], ln2_g[l0], ln2_b[l0], bt=32, lt=ls, row0=n_p, alpha=alpha)

    stack = lambda a: a[None]
    return (y_p.astype(x_prompt.dtype), y_s.astype(x_sample.dtype),
            *[stack(a) for a in st_p], *[stack(a) for a in st_s])
```
